```python
import jax, jax.numpy as jnp
from jax import lax
import numpy as np

D_MODEL = 2048
BATCH = 2
SEQ = 16384
DEPTH = 2

N_MIXERS = 2
RET_HEADS = 8
RET_QK_DIM = D_MODEL // RET_HEADS
RET_V_WIDTH = 2 * D_MODEL
RET_V_DIM = RET_V_WIDTH // RET_HEADS
CHUNK = 128
ROPE_BASE = 10000.0
CONV_WIDTH = 31
FFN_HIDDEN = -(-8 * D_MODEL // (3 * 256)) * 256
PLE_DIM = 256
N_RET = (DEPTH + 1) // 2
N_CONV = DEPTH // 2
EPS = 1e-6

kernel_name = "hybrid_retention_conformer_block"


def rms_norm(x, g):
    xf = x.astype(jnp.float32)
    y = xf * lax.rsqrt(jnp.mean(xf * xf, axis=-1, keepdims=True) + EPS)
    return (y * g.astype(jnp.float32)).astype(x.dtype)


def layer_norm(x, g, b):
    xf = x.astype(jnp.float32)
    xc = xf - jnp.mean(xf, axis=-1, keepdims=True)
    var = jnp.mean(xc * xc, axis=-1, keepdims=True)
    y = xc * lax.rsqrt(var + EPS) * g.astype(jnp.float32) + b.astype(jnp.float32)
    return y.astype(x.dtype)


def rotary(t, positions):
    half = t.shape[-1] // 2
    inv_freq = ROPE_BASE ** (-jnp.arange(half, dtype=jnp.float32) / half)
    ang = positions.astype(jnp.float32)[..., None] * inv_freq
    cos = jnp.cos(ang)[:, :, None, :]
    sin = jnp.sin(ang)[:, :, None, :]
    t1, t2 = t[..., :half], t[..., half:]
    return jnp.concatenate([t1 * cos - t2 * sin, t1 * sin + t2 * cos], axis=-1)


def chunkwise_retention(q, k, v):
    b, s, h, dk = q.shape
    dv = v.shape[-1]
    n_chunks = s // CHUNK
    log_gamma = jnp.log1p(-jnp.exp2(-5.0 - jnp.arange(RET_HEADS, dtype=jnp.float32)))
    idx = jnp.arange(CHUNK, dtype=jnp.float32)
    rel = idx[:, None] - idx[None, :]
    intra = jnp.where(rel >= 0, jnp.exp(log_gamma[:, None, None] * jnp.maximum(rel, 0.0)), 0.0)
    q_decay = jnp.exp(log_gamma[:, None] * (idx + 1.0))[..., None]
    k_decay = jnp.exp(log_gamma[:, None] * (CHUNK - 1.0 - idx))[..., None]
    state_decay = jnp.exp(log_gamma * CHUNK)[:, None, None]

    def to_chunks(t):
        return t.reshape(b, n_chunks, CHUNK, h, t.shape[-1]).transpose(1, 0, 3, 2, 4)

    def step(state, qkv):
        qc, kc, vc = qkv
        scores = jnp.einsum('bhid,bhjd->bhij', qc, kc) * intra
        out = (jnp.einsum('bhij,bhjv->bhiv', scores, vc)
               + jnp.einsum('bhid,bhdv->bhiv', qc, state) * q_decay)
        state = state * state_decay + jnp.einsum('bhjd,bhjv->bhdv', kc * k_decay, vc)
        return state, out

    state0 = jnp.zeros((b, h, dk, dv), jnp.float32)
    _, out = lax.scan(step, state0, (to_chunks(q), to_chunks(k), to_chunks(v)))
    return out.transpose(1, 0, 3, 2, 4).reshape(b, s, h, dv)


def retention_mixer(hn, w_in, w_out, positions):
    b, s, _ = hn.shape
    proj = (hn @ w_in).astype(jnp.float32)
    q, k, v, g = jnp.split(proj, [D_MODEL, 2 * D_MODEL, 2 * D_MODEL + RET_V_WIDTH], axis=-1)
    q = rotary(q.reshape(b, s, RET_HEADS, RET_QK_DIM), positions)
    k = rotary(k.reshape(b, s, RET_HEADS, RET_QK_DIM), positions) * (RET_QK_DIM ** -0.5)
    v = v.reshape(b, s, RET_HEADS, RET_V_DIM)
    o = chunkwise_retention(q, k, v)
    o = o * lax.rsqrt(jnp.mean(o * o, axis=-1, keepdims=True) + EPS)
    y = (jax.nn.silu(g) * o.reshape(b, s, RET_V_WIDTH)).astype(hn.dtype)
    return y @ w_out


def conformer_conv(hn, w_pw1, b_pw1, w_dw, b_dw, ln_g, ln_b, w_pw2, b_pw2):
    a = hn @ w_pw1 + b_pw1
    u = a[..., :D_MODEL] * jax.nn.sigmoid(a[..., D_MODEL:])
    u = lax.conv_general_dilated(
        u, w_dw[:, None, :], window_strides=(1,), padding=[(CONV_WIDTH - 1, 0)],
        dimension_numbers=('NWC', 'WIO', 'NWC'), feature_group_count=D_MODEL) + b_dw
    u = jax.nn.silu(layer_norm(u, ln_g, ln_b))
    return u @ w_pw2 + b_pw2


def swiglu(hn, w_gate, w_up, w_down):
    return (jax.nn.silu(hn @ w_gate) * (hn @ w_up)) @ w_down


def setup_inputs(seed: int = 0) -> dict:
    key = jax.random.key(seed)
    ks = jax.random.split(key, 24)

    def nrm(k, shape, scale):
        return scale * jax.random.normal(k, shape, jnp.float32)

    def gain(k, shape):
        return 1.0 + nrm(k, shape, 0.05)

    D, F = D_MODEL, FFN_HIDDEN
    return {
        'x': nrm(ks[0], (BATCH, SEQ, D), 1.0),
        'p': nrm(ks[1], (DEPTH, BATCH, SEQ, PLE_DIM), 1.0),
        'positions': jnp.broadcast_to(jnp.arange(SEQ, dtype=jnp.int32)[None, :], (BATCH, SEQ)),
        'ret_w_in': nrm(ks[2], (N_RET, D, 2 * D + 2 * RET_V_WIDTH), D ** -0.5),
        'ret_w_out': nrm(ks[3], (N_RET, RET_V_WIDTH, D), RET_V_WIDTH ** -0.5),
        'conv_w_pw1': nrm(ks[4], (N_CONV, D, 2 * D), D ** -0.5),
        'conv_b_pw1': nrm(ks[5], (N_CONV, 2 * D), 0.02),
        'conv_w_dw': nrm(ks[6], (N_CONV, CONV_WIDTH, D), CONV_WIDTH ** -0.5),
        'conv_b_dw': nrm(ks[7], (N_CONV, D), 0.02),
        'conv_ln_g': gain(ks[8], (N_CONV, D)),
        'conv_ln_b': nrm(ks[9], (N_CONV, D), 0.02),
        'conv_w_pw2': nrm(ks[10], (N_CONV, D, D), D ** -0.5),
        'conv_b_pw2': nrm(ks[11], (N_CONV, D), 0.02),
        'g_mix_pre': gain(ks[12], (DEPTH, D)),
        'g_mix_post': gain(ks[13], (DEPTH, D)),
        'ffn_w_gate': nrm(ks[14], (DEPTH, D, F), D ** -0.5),
        'ffn_w_up': nrm(ks[15], (DEPTH, D, F), D ** -0.5),
        'ffn_w_down': nrm(ks[16], (DEPTH, F, D), F ** -0.5),
        'g_ffn_pre': gain(ks[17], (DEPTH, D)),
        'g_ffn_post': gain(ks[18], (DEPTH, D)),
        'ple_w_proj': nrm(ks[19], (DEPTH, PLE_DIM, D), PLE_DIM ** -0.5),
        'ple_w_gate': nrm(ks[20], (DEPTH, D, D), D ** -0.5),
        'g_ple': gain(ks[21], (DEPTH, D)),
    }


def reference(x, p, positions, ret_w_in, ret_w_out, conv_w_pw1, conv_b_pw1, conv_w_dw,
              conv_b_dw, conv_ln_g, conv_ln_b, conv_w_pw2, conv_b_pw2, g_mix_pre, g_mix_post,
              ffn_w_gate, ffn_w_up, ffn_w_down, g_ffn_pre, g_ffn_post, ple_w_proj, ple_w_gate,
              g_ple):
    h = x
    for i in range(DEPTH):
        j = i // N_MIXERS
        hn = rms_norm(h, g_mix_pre[i])
        if i % N_MIXERS == 0:
            y = retention_mixer(hn, ret_w_in[j], ret_w_out[j], positions)
        else:
            y = conformer_conv(hn, conv_w_pw1[j], conv_b_pw1[j], conv_w_dw[j], conv_b_dw[j],
                               conv_ln_g[j], conv_ln_b[j], conv_w_pw2[j], conv_b_pw2[j])
        h = h + rms_norm(y, g_mix_post[i])
        f = swiglu(rms_norm(h, g_ffn_pre[i]), ffn_w_gate[i], ffn_w_up[i], ffn_w_down[i])
        h = h + rms_norm(f, g_ffn_post[i])
        gate = jax.nn.sigmoid(h @ ple_w_gate[i])
        h = h + rms_norm(gate * (p[i] @ ple_w_proj[i]), g_ple[i])
    return h
```

```python
import functools

import jax
import jax.numpy as jnp
from jax import lax
from jax.experimental import pallas as pl
from jax.experimental.pallas import tpu as pltpu

EPS = 1e-6
ROPE_BASE = 10000.0
RET_HEADS = 8
RET_CHUNK = 256
CONV_HALO = 32
PROLOGUE_ROWS = 128

F32 = jnp.float32
BF16 = jnp.bfloat16

LANES = 128
VMEM_LIMIT_BYTES = 56 * 1024 * 1024


def _params(*semantics):
    return pltpu.CompilerParams(dimension_semantics=semantics, vmem_limit_bytes=VMEM_LIMIT_BYTES)


def _rms(x):
    return x * lax.rsqrt(jnp.mean(x * x, axis=-1, keepdims=True) + EPS)


def _norm_rows_to(dst_ref, src_ref, gain_ref):
    rows = src_ref.shape[0]

    def body(r, carry):
        sl = pl.ds(pl.multiple_of(r * PROLOGUE_ROWS, PROLOGUE_ROWS), PROLOGUE_ROWS)
        dst_ref[sl, :] = (_rms(src_ref[sl, :]) * gain_ref[...]).astype(BF16)
        return carry

    lax.fori_loop(0, rows // PROLOGUE_ROWS, body, 0)


def _rotary_table_kernel(pos_ref, inv_freq_ref, cos_ref, sin_ref):
    ang = pos_ref[...] * inv_freq_ref[...]
    cos_ref[...] = jnp.cos(ang)
    sin_ref[...] = jnp.sin(ang)


def _rotary_tables(positions, half):
    t = positions.size
    rows = 2048
    inv_freq = (ROPE_BASE ** (-jnp.arange(half, dtype=F32) / half)).reshape(1, half)
    pos = jnp.broadcast_to(positions.reshape(t, 1).astype(F32), (t, half))
    spec = pl.BlockSpec((rows, half), lambda i: (i, 0))
    return pl.pallas_call(
        _rotary_table_kernel,
        grid=(t // rows,),
        in_specs=[spec, pl.BlockSpec((1, half), lambda i: (0, 0))],
        out_specs=[spec, spec],
        out_shape=[jax.ShapeDtypeStruct((t, half), F32)] * 2,
        compiler_params=_params("parallel"),
        name="rotary_tables",
    )(pos, inv_freq)


def _ret_inproj_kernel(h_ref, gain_ref, w_ref, cos_ref, sin_ref, o_ref, hn_ref, *, n_q, n_qk, dk):
    j = pl.program_id(1)

    @pl.when(j == 0)
    def _():
        _norm_rows_to(hn_ref, h_ref, gain_ref)

    acc = jnp.dot(hn_ref[...], w_ref[...], preferred_element_type=F32)
    half = dk // 2

    @pl.when(j < n_qk)
    def _():
        scale = jnp.where(j >= n_q, dk ** -0.5, 1.0).astype(F32)
        c = cos_ref[...] * scale
        s = sin_ref[...] * scale
        for hh in range(acc.shape[1] // dk):
            t1 = acc[:, hh * dk:hh * dk + half]
            t2 = acc[:, hh * dk + half:(hh + 1) * dk]
            o_ref[:, hh * dk:hh * dk + half] = (t1 * c - t2 * s).astype(BF16)
            o_ref[:, hh * dk + half:(hh + 1) * dk] = (t1 * s + t2 * c).astype(BF16)

    @pl.when(j >= n_qk)
    def _():
        o_ref[...] = acc.astype(BF16)


def _ret_inproj(h, gain, w_in, cos, sin, d, dk):
    t = h.shape[0]
    n = w_in.shape[1]
    bm, bn = 1024, 1024
    kern = functools.partial(_ret_inproj_kernel, n_q=d // bn, n_qk=2 * d // bn, dk=dk)
    return pl.pallas_call(
        kern,
        grid=(t // bm, n // bn),
        in_specs=[
            pl.BlockSpec((bm, d), lambda i, j: (i, 0)),
            pl.BlockSpec((1, d), lambda i, j: (0, 0)),
            pl.BlockSpec((d, bn), lambda i, j: (0, j)),
            pl.BlockSpec((bm, dk // 2), lambda i, j: (i, 0)),
            pl.BlockSpec((bm, dk // 2), lambda i, j: (i, 0)),
        ],
        out_specs=pl.BlockSpec((bm, bn), lambda i, j: (i, j)),
        out_shape=jax.ShapeDtypeStruct((t, n), BF16),
        scratch_shapes=[pltpu.VMEM((bm, d), BF16)],
        compiler_params=_params("parallel", "arbitrary"),
        name="ret_inproj",
    )(h, gain, w_in, cos, sin)


def _retention_kernel(sdec_ref, q_ref, k_ref, v_ref, g_ref, intra_ref, qdec_ref, kdec_ref, y_ref, state_ref,
                      *, chunk):
    hd = pl.program_id(1)

    @pl.when(pl.program_id(2) == 0)
    def _():
        state_ref[...] = jnp.zeros_like(state_ref)

    sdec = sdec_ref[hd]
    intra = intra_ref[0]
    qdec = qdec_ref[0]
    kdec = kdec_ref[0]
    for c in range(q_ref.shape[0] // chunk):
        sl = slice(c * chunk, (c + 1) * chunk)
        q = q_ref[sl, :]
        k = k_ref[sl, :]
        v = v_ref[sl, :]
        state = state_ref[...]
        scores = lax.dot_general(q, k, (((1,), (1,)), ((), ())), preferred_element_type=F32)
        p = (scores * intra).astype(BF16)
        o = jnp.dot(p, v, preferred_element_type=F32)
        o = o + jnp.dot(q, state.astype(BF16), preferred_element_type=F32) * qdec
        kd = (k.astype(F32) * kdec).astype(BF16)
        state_ref[...] = state * sdec + lax.dot_general(
            kd, v, (((0,), (0,)), ((), ())), preferred_element_type=F32)
        g = g_ref[sl, :].astype(F32)
        y_ref[sl, :] = (g * jax.nn.sigmoid(g) * _rms(o)).astype(BF16)


def _retention(proj, batch, seq, d, dk, dv):
    t = proj.shape[0]
    heads = d // dk
    chunk = RET_CHUNK
    tb = 1024
    nsb = seq // tb
    log_gamma = jnp.log1p(-jnp.exp2(-5.0 - jnp.arange(heads, dtype=F32)))
    idx = jnp.arange(chunk, dtype=F32)
    rel = idx[:, None] - idx[None, :]
    intra = jnp.where(rel >= 0, jnp.exp(log_gamma[:, None, None] * jnp.maximum(rel, 0.0)), 0.0)
    qdec = jnp.broadcast_to(jnp.exp(log_gamma[:, None] * (idx + 1.0))[..., None], (heads, chunk, dv))
    kdec = jnp.broadcast_to(jnp.exp(log_gamma[:, None] * (chunk - 1.0 - idx))[..., None], (heads, chunk, dk))
    sdec = jnp.exp(log_gamma * chunk)

    kern = functools.partial(_retention_kernel, chunk=chunk)
    koff = d // dk
    voff = 2 * d // dv
    goff = voff + heads
    return pl.pallas_call(
        kern,
        grid=(batch, heads, nsb),
        in_specs=[
            pl.BlockSpec(memory_space=pltpu.SMEM),
            pl.BlockSpec((tb, dk), lambda b, h, s: (b * nsb + s, h)),
            pl.BlockSpec((tb, dk), lambda b, h, s: (b * nsb + s, koff + h)),
            pl.BlockSpec((tb, dv), lambda b, h, s: (b * nsb + s, voff + h)),
            pl.BlockSpec((tb, dv), lambda b, h, s: (b * nsb + s, goff + h)),
            pl.BlockSpec((1, chunk, chunk), lambda b, h, s: (h, 0, 0)),
            pl.BlockSpec((1, chunk, dv), lambda b, h, s: (h, 0, 0)),
            pl.BlockSpec((1, chunk, dk), lambda b, h, s: (h, 0, 0)),
        ],
        out_specs=pl.BlockSpec((tb, dv), lambda b, h, s: (b * nsb + s, h)),
        out_shape=jax.ShapeDtypeStruct((t, heads * dv), BF16),
        scratch_shapes=[pltpu.VMEM((dk, dv), F32)],
        compiler_params=_params("parallel", "parallel", "arbitrary"),
        name="retention",
    )(sdec, proj, proj, proj, proj, intra, qdec, kdec)


def _proj_norm_res_kernel(x_ref, w_ref, h_ref, gain_ref, o_ref):
    y = jnp.dot(x_ref[...], w_ref[...], preferred_element_type=F32)
    o_ref[...] = h_ref[...] + _rms(y) * gain_ref[...]


def _proj_norm_res(x, w, h, gain):
    t, k = x.shape
    d = w.shape[1]
    bm = 512
    return pl.pallas_call(
        _proj_norm_res_kernel,
        grid=(t // bm,),
        in_specs=[
            pl.BlockSpec((bm, k), lambda i: (i, 0)),
            pl.BlockSpec((k, d), lambda i: (0, 0), pipeline_mode=pl.Buffered(1)),
            pl.BlockSpec((bm, d), lambda i: (i, 0)),
            pl.BlockSpec((1, d), lambda i: (0, 0)),
        ],
        out_specs=pl.BlockSpec((bm, d), lambda i: (i, 0)),
        out_shape=jax.ShapeDtypeStruct((t, d), F32),
        compiler_params=_params("parallel"),
        name="proj_norm_res",
    )(x, w, h, gain)


def _ffn_kernel(h_ref, gpre_ref, wg_ref, wu_ref, wd_ref, gpost_ref, o_ref, hn_ref, acc_ref):
    f = pl.program_id(1)

    @pl.when(f == 0)
    def _():
        _norm_rows_to(hn_ref, h_ref, gpre_ref)
        acc_ref[...] = jnp.zeros_like(acc_ref)

    hn = hn_ref[...]
    g = jnp.dot(hn, wg_ref[...], preferred_element_type=F32)
    u = jnp.dot(hn, wu_ref[...], preferred_element_type=F32)
    a = (g * jax.nn.sigmoid(g) * u).astype(BF16)
    acc_ref[...] += jnp.dot(a, wd_ref[...], preferred_element_type=F32)

    @pl.when(f == pl.num_programs(1) - 1)
    def _():
        o_ref[...] = h_ref[...] + _rms(acc_ref[...]) * gpost_ref[...]


def _ffn(h, gpre, wg, wu, wd, gpost):
    t, d = h.shape
    hidden = wg.shape[1]
    bm, bf = 512, 512
    row = lambda i, f: (i, 0)
    vec = lambda i, f: (0, 0)
    return pl.pallas_call(
        _ffn_kernel,
        grid=(t // bm, hidden // bf),
        in_specs=[
            pl.BlockSpec((bm, d), row),
            pl.BlockSpec((1, d), vec),
            pl.BlockSpec((d, bf), lambda i, f: (0, f)),
            pl.BlockSpec((d, bf), lambda i, f: (0, f)),
            pl.BlockSpec((bf, d), lambda i, f: (f, 0)),
            pl.BlockSpec((1, d), vec),
        ],
        out_specs=pl.BlockSpec((bm, d), row),
        out_shape=jax.ShapeDtypeStruct((t, d), F32),
        scratch_shapes=[pltpu.VMEM((bm, d), BF16), pltpu.VMEM((bm, d), F32)],
        compiler_params=_params("parallel", "arbitrary"),
        name="ffn",
    )(h, gpre, wg, wu, wd, gpost)


def _ple_kernel(h_ref, p_ref, wgate_ref, wproj_ref, gain_ref, o_ref):
    h = h_ref[...]
    gate = jax.nn.sigmoid(jnp.dot(h.astype(BF16), wgate_ref[...], preferred_element_type=F32))
    emb = jnp.dot(p_ref[...].astype(BF16), wproj_ref[...], preferred_element_type=F32)
    o_ref[...] = h + _rms(gate * emb) * gain_ref[...]


def _ple(h, p, wgate, wproj, gain):
    t, d = h.shape
    pd = p.shape[1]
    bm = 512
    return pl.pallas_call(
        _ple_kernel,
        grid=(t // bm,),
        in_specs=[
            pl.BlockSpec((bm, d), lambda i: (i, 0)),
            pl.BlockSpec((bm, pd), lambda i: (i, 0)),
            pl.BlockSpec((d, d), lambda i: (0, 0), pipeline_mode=pl.Buffered(1)),
            pl.BlockSpec((pd, d), lambda i: (0, 0), pipeline_mode=pl.Buffered(1)),
            pl.BlockSpec((1, d), lambda i: (0, 0)),
        ],
        out_specs=pl.BlockSpec((bm, d), lambda i: (i, 0)),
        out_shape=jax.ShapeDtypeStruct((t, d), F32),
        compiler_params=_params("parallel"),
        name="ple",
    )(h, p, wgate, wproj, gain)


def _glu_kernel(h_ref, gain_ref, wa_ref, wb_ref, ba_ref, bb_ref, o_ref, hn_ref):
    @pl.when(pl.program_id(1) == 0)
    def _():
        _norm_rows_to(hn_ref, h_ref, gain_ref)

    hn = hn_ref[...]
    a = jnp.dot(hn, wa_ref[...], preferred_element_type=F32) + ba_ref[...]
    b = jnp.dot(hn, wb_ref[...], preferred_element_type=F32) + bb_ref[...]
    u = a * jax.nn.sigmoid(b)
    for s in range(o_ref.shape[0]):
        o_ref[s] = u[:, s * LANES:(s + 1) * LANES]


def _glu(h, gain, w, bias):
    t, d = h.shape
    bm, bn = 1024, 512
    nb = d // bn
    spb = bn // LANES
    return pl.pallas_call(
        _glu_kernel,
        grid=(t // bm, nb),
        in_specs=[
            pl.BlockSpec((bm, d), lambda i, j: (i, 0)),
            pl.BlockSpec((1, d), lambda i, j: (0, 0)),
            pl.BlockSpec((d, bn), lambda i, j: (0, j)),
            pl.BlockSpec((d, bn), lambda i, j: (0, nb + j)),
            pl.BlockSpec((1, bn), lambda i, j: (0, j)),
            pl.BlockSpec((1, bn), lambda i, j: (0, nb + j)),
        ],
        out_specs=pl.BlockSpec((spb, bm, LANES), lambda i, j: (j, i, 0)),
        out_shape=jax.ShapeDtypeStruct((d // LANES, t, LANES), F32),
        scratch_shapes=[pltpu.VMEM((bm, d), BF16)],
        compiler_params=_params("parallel", "arbitrary"),
        name="conv_glu",
    )(h, gain, w, w, bias, bias)


def _conv_tail_kernel(u_ref, halo_ref, wdw_ref, bdw_ref, lng_ref, lnb_ref, w2_ref, b2_ref, h_ref, gain_ref, o_ref,
                      ubuf_ref, cbuf_ref, zbuf_ref, *, width, blocks_per_seq, row_chunk, ln_rows):
    nslab, bm, lanes = u_ref.shape
    halo = halo_ref.shape[1]
    first = (pl.program_id(0) % blocks_per_seq) == 0
    ubuf_ref[:, 0:halo, :] = jnp.where(first, 0.0, halo_ref[...])
    ubuf_ref[:, halo:, :] = u_ref[...]
    base = halo - (width - 1)

    def slab(c, carry):
        w = wdw_ref[c]
        bias = bdw_ref[c]
        for r in range(0, bm, row_chunk):
            acc = jnp.broadcast_to(bias, (row_chunk, lanes))
            for k in range(width):
                acc = acc + ubuf_ref[c, r + base + k:r + base + k + row_chunk, :] * w[k:k + 1, :]
            cbuf_ref[c, r:r + row_chunk, :] = acc
        return carry

    lax.fori_loop(0, nslab, slab, 0)

    inv_d = 1.0 / (nslab * lanes)

    def ln_rows_body(rb, carry):
        rs = pl.ds(pl.multiple_of(rb * ln_rows, ln_rows), ln_rows)
        x = cbuf_ref[:, rs, :]
        mean = jnp.sum(jnp.sum(x, axis=0), axis=-1, keepdims=True) * inv_d
        xc = x - mean
        var = jnp.sum(jnp.sum(xc * xc, axis=0), axis=-1, keepdims=True) * inv_d
        z = xc * lax.rsqrt(var + EPS) * lng_ref[...] + lnb_ref[...]
        z = z * jax.nn.sigmoid(z)
        for s in range(nslab):
            zbuf_ref[rs, s * lanes:(s + 1) * lanes] = z[s].astype(BF16)
        return carry

    lax.fori_loop(0, bm // ln_rows, ln_rows_body, 0)

    y = jnp.dot(zbuf_ref[...], w2_ref[...], preferred_element_type=F32) + b2_ref[...]
    o_ref[...] = h_ref[...] + _rms(y) * gain_ref[...]


def _conv_tail(u, wdw, bdw, lng, lnb, w2, b2, h, gain, seq):
    nslab, t, lanes = u.shape
    d = nslab * lanes
    width = wdw.shape[1]
    bm = 256
    hb = bm // CONV_HALO
    kern = functools.partial(_conv_tail_kernel, width=width, blocks_per_seq=seq // bm, row_chunk=64, ln_rows=64)
    vec = pl.BlockSpec((1, d), lambda i: (0, 0))
    slabvec = pl.BlockSpec((nslab, 1, lanes), lambda i: (0, 0, 0))
    return pl.pallas_call(
        kern,
        grid=(t // bm,),
        in_specs=[
            pl.BlockSpec((nslab, bm, lanes), lambda i: (0, i, 0)),
            pl.BlockSpec((nslab, CONV_HALO, lanes), lambda i: (0, jnp.maximum(i * hb - 1, 0), 0)),
            pl.BlockSpec((nslab, width, lanes), lambda i: (0, 0, 0)),
            slabvec, slabvec, slabvec,
            pl.BlockSpec((d, d), lambda i: (0, 0), pipeline_mode=pl.Buffered(1)),
            vec,
            pl.BlockSpec((bm, d), lambda i: (i, 0)),
            vec,
        ],
        out_specs=pl.BlockSpec((bm, d), lambda i: (i, 0)),
        out_shape=jax.ShapeDtypeStruct((t, d), F32),
        scratch_shapes=[pltpu.VMEM((nslab, bm + CONV_HALO, lanes), F32), pltpu.VMEM((nslab, bm, lanes), F32),
                        pltpu.VMEM((bm, d), BF16)],
        compiler_params=_params("arbitrary"),
        name="conv_tail",
    )(u, u, wdw, bdw, lng, lnb, w2, b2, h, gain)


def kernel(x, p, positions, ret_w_in, ret_w_out, conv_w_pw1, conv_b_pw1, conv_w_dw, conv_b_dw, conv_ln_g, conv_ln_b, conv_w_pw2, conv_b_pw2, g_mix_pre, g_mix_post, ffn_w_gate, ffn_w_up, ffn_w_down, g_ffn_pre, g_ffn_post, ple_w_proj, ple_w_gate, g_ple):
    batch, seq, d = x.shape
    t = batch * seq
    depth = p.shape[0]
    dk = d // RET_HEADS
    dv = ret_w_out.shape[1] // RET_HEADS
    assert seq % 1024 == 0 and d % 1024 == 0 and conv_w_dw.shape[1] - 1 <= CONV_HALO

    row = lambda v: v.reshape(1, -1)
    h = x.reshape(t, d)
    cos, sin = _rotary_tables(positions, dk // 2)
    for i in range(depth):
        j = i // 2
        if i % 2 == 0:
            proj = _ret_inproj(h, row(g_mix_pre[i]), ret_w_in[j].astype(BF16), cos, sin, d, dk)
            y = _retention(proj, batch, seq, d, dk, dv)
            h = _proj_norm_res(y, ret_w_out[j].astype(BF16), h, row(g_mix_post[i]))
        else:
            slabs = lambda v: v.reshape(-1, d // LANES, LANES).transpose(1, 0, 2)
            u = _glu(h, row(g_mix_pre[i]), conv_w_pw1[j].astype(BF16), row(conv_b_pw1[j]))
            h = _conv_tail(u, slabs(conv_w_dw[j]), slabs(conv_b_dw[j]), slabs(conv_ln_g[j]), slabs(conv_ln_b[j]),
                           conv_w_pw2[j].astype(BF16), row(conv_b_pw2[j]), h, row(g_mix_post[i]), seq)
        h = _ffn(h, row(g_ffn_pre[i]), ffn_w_gate[i].astype(BF16), ffn_w_up[i].astype(BF16),
                 ffn_w_down[i].astype(BF16), row(g_ffn_post[i]))
        h = _ple(h, p[i].reshape(t, -1), ple_w_gate[i].astype(BF16), ple_w_proj[i].astype(BF16), row(g_ple[i]))
    return h.reshape(batch, seq, d)
```
